```python
import math
import jax, jax.numpy as jnp
from jax import lax
import numpy as np

D_MODEL = 1024
BATCH = 8
SEQ = 4096
DEPTH = 4

N_MIXERS = 2
N_GDN_LAYERS = (DEPTH + 1) // 2
N_SB_LAYERS = DEPTH // 2

GDN_HEAD_DIM = 128
GDN_QK_HEADS = D_MODEL // 128
GDN_V_HEADS = 2 * GDN_QK_HEADS
GDN_KEY_DIM = GDN_QK_HEADS * GDN_HEAD_DIM
GDN_VAL_DIM = GDN_V_HEADS * GDN_HEAD_DIM
GDN_CONV_K = 4
GDN_CONV_DIM = 2 * GDN_KEY_DIM + GDN_VAL_DIM
GDN_IN_DIM = GDN_CONV_DIM + GDN_VAL_DIM + 2 * GDN_V_HEADS
GDN_CHUNK = 64

SB_HEAD_DIM = 128
SB_HEADS = D_MODEL // SB_HEAD_DIM
SB_DIM = SB_HEADS * SB_HEAD_DIM
SB_BLOCK = 128

FF_DIM = 7 * D_MODEL // 2
N_EXPERTS = 8
TOP_K = 2

DEEPNORM_ALPHA = (2.0 * DEPTH) ** 0.25
DEEPNORM_BETA = (8.0 * DEPTH) ** -0.25
LN_EPS = 1e-5
RMS_EPS = 1e-6

kernel_name = 'hybrid_gdn_stickbreak_moe_deepnorm'


def layer_norm(x, g, b):
    xf = x.astype(jnp.float32)
    mu = xf.mean(-1, keepdims=True)
    var = jnp.square(xf - mu).mean(-1, keepdims=True)
    y = (xf - mu) * lax.rsqrt(var + LN_EPS) * g.astype(jnp.float32) + b.astype(jnp.float32)
    return y.astype(x.dtype)


def l2norm(x):
    xf = x.astype(jnp.float32)
    return xf * lax.rsqrt(jnp.sum(xf * xf, -1, keepdims=True) + RMS_EPS)


def causal_short_conv(x, w):
    K = w.shape[0]
    S = x.shape[1]
    xp = jnp.pad(x, ((0, 0), (K - 1, 0), (0, 0)))
    y = w[0] * xp[:, 0:S]
    for j in range(1, K):
        y = y + w[j] * xp[:, j:j + S]
    return jax.nn.silu(y)


def chunk_gated_delta_rule(q, k, v, g, beta):
    B, H, S, dk = q.shape
    dv = v.shape[-1]
    C = GDN_CHUNK
    N = S // C
    q = q * (dk ** -0.5)
    qc = q.reshape(B, H, N, C, dk)
    kc = k.reshape(B, H, N, C, dk)
    vc = v.reshape(B, H, N, C, dv)
    bc = beta.reshape(B, H, N, C)
    gcum = jnp.cumsum(g.reshape(B, H, N, C), axis=-1)
    tri = jnp.tril(jnp.ones((C, C), dtype=bool))
    strict = jnp.tril(jnp.ones((C, C), dtype=bool), -1)
    diff = gcum[..., :, None] - gcum[..., None, :]
    decay = jnp.where(tri, jnp.exp(jnp.where(tri, diff, 0.0)), 0.0)
    kb = kc * bc[..., None]
    L = jnp.where(strict, jnp.einsum('bhnid,bhnjd->bhnij', kb, kc) * decay, 0.0)
    eye = jnp.eye(C, dtype=jnp.float32)
    T = lax.linalg.triangular_solve(eye + L, jnp.broadcast_to(eye, L.shape),
                                    left_side=True, lower=True, unit_diagonal=True)
    u = jnp.einsum('bhnij,bhnjd->bhnid', T, vc * bc[..., None])
    w = jnp.einsum('bhnij,bhnjd->bhnid', T, kb * jnp.exp(gcum)[..., None])
    intra = jnp.where(tri, jnp.einsum('bhnid,bhnjd->bhnij', qc, kc) * decay, 0.0)

    def step(state, inp):
        q_i, k_i, u_i, w_i, g_i, a_i = inp
        v_new = u_i - jnp.einsum('bhcd,bhde->bhce', w_i, state)
        o = (jnp.einsum('bhcd,bhde->bhce', q_i * jnp.exp(g_i)[..., None], state)
             + jnp.einsum('bhij,bhje->bhie', a_i, v_new))
        g_last = g_i[..., -1]
        k_dec = k_i * jnp.exp(g_last[..., None] - g_i)[..., None]
        state = state * jnp.exp(g_last)[..., None, None] + jnp.einsum('bhcd,bhce->bhde', k_dec, v_new)
        return state, o

    xs = (qc, kc, u, w, gcum, intra)
    xs = tuple(jnp.moveaxis(a, 2, 0) for a in xs)
    state0 = jnp.zeros((B, H, dk, dv), jnp.float32)
    _, o = lax.scan(step, state0, xs)
    return jnp.moveaxis(o, 0, 2).reshape(B, H, S, dv)


def gated_deltanet(x, w_in, conv_w, a_log, dt_bias, norm_w, w_out):
    B, S, _ = x.shape
    proj = x @ w_in
    o1 = GDN_CONV_DIM
    o2 = o1 + GDN_VAL_DIM
    o3 = o2 + GDN_V_HEADS
    qkv, z, b, a = proj[..., :o1], proj[..., o1:o2], proj[..., o2:o3], proj[..., o3:]
    qkv = causal_short_conv(qkv, conv_w)
    q = qkv[..., :GDN_KEY_DIM].reshape(B, S, GDN_QK_HEADS, GDN_HEAD_DIM)
    k = qkv[..., GDN_KEY_DIM:2 * GDN_KEY_DIM].reshape(B, S, GDN_QK_HEADS, GDN_HEAD_DIM)
    v = qkv[..., 2 * GDN_KEY_DIM:].reshape(B, S, GDN_V_HEADS, GDN_HEAD_DIM).astype(jnp.float32)
    rep = GDN_V_HEADS // GDN_QK_HEADS
    q = jnp.repeat(l2norm(q), rep, axis=2)
    k = jnp.repeat(l2norm(k), rep, axis=2)
    beta = jax.nn.sigmoid(b.astype(jnp.float32))
    g = -jnp.exp(a_log.astype(jnp.float32)) * jax.nn.softplus(a.astype(jnp.float32) + dt_bias.astype(jnp.float32))
    to_bhs = lambda t: jnp.swapaxes(t, 1, 2)
    o = chunk_gated_delta_rule(to_bhs(q), to_bhs(k), to_bhs(v), to_bhs(g), to_bhs(beta))
    o = jnp.swapaxes(o, 1, 2)
    o = o * lax.rsqrt(jnp.mean(o * o, -1, keepdims=True) + RMS_EPS) * norm_w.astype(jnp.float32)
    o = o * jax.nn.silu(z.reshape(B, S, GDN_V_HEADS, GDN_HEAD_DIM).astype(jnp.float32))
    return o.reshape(B, S, GDN_VAL_DIM).astype(x.dtype) @ w_out


def stick_breaking_attention(x, w_in, w_out):
    B, S, _ = x.shape
    proj = x @ w_in
    heads = lambda t: t.reshape(B, S, SB_HEADS, SB_HEAD_DIM).transpose(0, 2, 1, 3)
    q = heads(proj[..., :SB_DIM])
    k = heads(proj[..., SB_DIM:2 * SB_DIM])
    v = heads(proj[..., 2 * SB_DIM:])
    scale = SB_HEAD_DIM ** -0.5
    outs = []
    for i in range(S // SB_BLOCK):
        lo, hi = i * SB_BLOCK, (i + 1) * SB_BLOCK
        z = jnp.einsum('bhtd,bhsd->bhts', q[:, :, lo:hi], k[:, :, :hi]).astype(jnp.float32) * scale
        t_pos = lo + jnp.arange(SB_BLOCK)[:, None]
        s_pos = jnp.arange(hi)[None, :]
        causal = s_pos < t_pos
        log_keep = jnp.where(causal, jax.nn.log_sigmoid(-z), 0.0)
        later = lax.cumsum(log_keep, axis=3, reverse=True) - log_keep
        attn = jnp.where(causal, jnp.exp(jax.nn.log_sigmoid(z) + later), 0.0)
        outs.append(jnp.einsum('bhts,bhsd->bhtd', attn.astype(v.dtype), v[:, :, :hi]))
    o = jnp.concatenate(outs, axis=2)
    return o.transpose(0, 2, 1, 3).reshape(B, S, SB_DIM) @ w_out


def swiglu(x, w_gate, w_up, w_down):
    return (jax.nn.silu(x @ w_gate) * (x @ w_up)) @ w_down


def moe_swiglu(x, router, w_gate, w_up, w_down):
    B, S, D = x.shape
    xt = x.reshape(-1, D)
    logits = (xt @ router).astype(jnp.float32)
    top_val, top_idx = lax.top_k(logits, TOP_K)
    gates = jax.nn.softmax(top_val, axis=-1)
    combine = jnp.sum(jax.nn.one_hot(top_idx, N_EXPERTS, dtype=jnp.float32) * gates[..., None], axis=1)
    combine = combine.astype(x.dtype)
    y = jnp.zeros_like(xt)
    for e in range(N_EXPERTS):
        y = y + combine[:, e:e + 1] * swiglu(xt, w_gate[e], w_up[e], w_down[e])
    return y.reshape(B, S, D)


def setup_inputs(seed: int = 0) -> dict:
    key = jax.random.key(seed)
    ks = jax.random.split(key, 20)
    f32 = jnp.float32

    def nrm(k, shape, scale):
        return jax.random.normal(k, shape, f32) * scale

    x = nrm(ks[0], (BATCH, SEQ, D_MODEL), 1.0)
    gdn_cols = jnp.ones((GDN_IN_DIM,), f32).at[2 * GDN_KEY_DIM:GDN_CONV_DIM].set(DEEPNORM_BETA)
    gdn_w_in = nrm(ks[1], (N_GDN_LAYERS, D_MODEL, GDN_IN_DIM), D_MODEL ** -0.5) * gdn_cols
    gdn_conv_w = nrm(ks[2], (N_GDN_LAYERS, GDN_CONV_K, GDN_CONV_DIM), GDN_CONV_K ** -0.5)
    gdn_a_log = jnp.log(jax.random.uniform(ks[3], (N_GDN_LAYERS, GDN_V_HEADS), f32, 1.0, 16.0))
    dt = jnp.exp(jax.random.uniform(ks[4], (N_GDN_LAYERS, GDN_V_HEADS), f32, math.log(1e-3), math.log(1e-1)))
    gdn_dt_bias = dt + jnp.log(-jnp.expm1(-dt))
    gdn_norm_w = 1.0 + nrm(ks[5], (N_GDN_LAYERS, GDN_HEAD_DIM), 0.02)
    gdn_w_out = nrm(ks[6], (N_GDN_LAYERS, GDN_VAL_DIM, D_MODEL), GDN_VAL_DIM ** -0.5 * DEEPNORM_BETA)
    sb_cols = jnp.ones((3 * SB_DIM,), f32).at[2 * SB_DIM:].set(DEEPNORM_BETA)
    sb_w_in = nrm(ks[7], (N_SB_LAYERS, D_MODEL, 3 * SB_DIM), D_MODEL ** -0.5) * sb_cols
    sb_w_out = nrm(ks[8], (N_SB_LAYERS, SB_DIM, D_MODEL), SB_DIM ** -0.5 * DEEPNORM_BETA)
    ffn_w_gate = nrm(ks[9], (N_GDN_LAYERS, D_MODEL, FF_DIM), D_MODEL ** -0.5)
    ffn_w_up = nrm(ks[10], (N_GDN_LAYERS, D_MODEL, FF_DIM), D_MODEL ** -0.5)
    ffn_w_down = nrm(ks[11], (N_GDN_LAYERS, FF_DIM, D_MODEL), FF_DIM ** -0.5 * DEEPNORM_BETA)
    moe_router = nrm(ks[12], (N_SB_LAYERS, D_MODEL, N_EXPERTS), D_MODEL ** -0.5)
    moe_w_gate = nrm(ks[13], (N_SB_LAYERS, N_EXPERTS, D_MODEL, FF_DIM), D_MODEL ** -0.5)
    moe_w_up = nrm(ks[14], (N_SB_LAYERS, N_EXPERTS, D_MODEL, FF_DIM), D_MODEL ** -0.5)
    moe_w_down = nrm(ks[15], (N_SB_LAYERS, N_EXPERTS, FF_DIM, D_MODEL), FF_DIM ** -0.5 * DEEPNORM_BETA)
    ln_g = 1.0 + nrm(ks[16], (DEPTH, 2, D_MODEL), 0.02)
    ln_b = nrm(ks[17], (DEPTH, 2, D_MODEL), 0.02)
    return {'x': x, 'gdn_w_in': gdn_w_in, 'gdn_conv_w': gdn_conv_w, 'gdn_a_log': gdn_a_log,
            'gdn_dt_bias': gdn_dt_bias, 'gdn_norm_w': gdn_norm_w, 'gdn_w_out': gdn_w_out,
            'sb_w_in': sb_w_in, 'sb_w_out': sb_w_out,
            'ffn_w_gate': ffn_w_gate, 'ffn_w_up': ffn_w_up, 'ffn_w_down': ffn_w_down,
            'moe_router': moe_router, 'moe_w_gate': moe_w_gate, 'moe_w_up': moe_w_up,
            'moe_w_down': moe_w_down, 'ln_g': ln_g, 'ln_b': ln_b}


def reference(x, gdn_w_in, gdn_conv_w, gdn_a_log, gdn_dt_bias, gdn_norm_w, gdn_w_out,
              sb_w_in, sb_w_out, ffn_w_gate, ffn_w_up, ffn_w_down,
              moe_router, moe_w_gate, moe_w_up, moe_w_down, ln_g, ln_b):
    h = x
    for i in range(DEPTH):
        j = i // N_MIXERS
        if i % N_MIXERS == 0:
            mix = gated_deltanet(h, gdn_w_in[j], gdn_conv_w[j], gdn_a_log[j], gdn_dt_bias[j],
                                 gdn_norm_w[j], gdn_w_out[j])
        else:
            mix = stick_breaking_attention(h, sb_w_in[j], sb_w_out[j])
        h = layer_norm(DEEPNORM_ALPHA * h + mix, ln_g[i, 0], ln_b[i, 0])
        if i % 2 == 0:
            ff = swiglu(h, ffn_w_gate[j], ffn_w_up[j], ffn_w_down[j])
        else:
            ff = moe_swiglu(h, moe_router[j], moe_w_gate[j], moe_w_up[j], moe_w_down[j])
        h = layer_norm(DEEPNORM_ALPHA * h + ff, ln_g[i, 1], ln_b[i, 1])
    return h
```

```python
import functools

import jax
import jax.numpy as jnp
from jax import lax
from jax.experimental import pallas as pl
from jax.experimental.pallas import tpu as pltpu

F32 = jnp.float32
BF16 = jnp.bfloat16

DEPTH = 4
DEEPNORM_ALPHA = (2.0 * DEPTH) ** 0.25
LN_EPS = 1e-5
RMS_EPS = 1e-6

HEAD_DIM = 128
GDN_CHUNK = 64
GDN_CONV_K = 4
N_EXPERTS = 8
LANES = 128
SB_TQ = 256
SB_TK = 128
EXP_UNDERFLOW = -104.0

VMEM_LIMIT = 56 * 1024 * 1024


def _cparams(sem):
    return pltpu.CompilerParams(dimension_semantics=sem, vmem_limit_bytes=VMEM_LIMIT)


def _dot(a, b):
    return jnp.dot(a, b, preferred_element_type=F32)


def _dot_nt(a, b):
    return lax.dot_general(a, b, (((1,), (1,)), ((), ())), preferred_element_type=F32)


def _split(a):
    hi = a.astype(BF16)
    lo = (a - hi.astype(F32)).astype(BF16)
    return hi, lo


def _dot3(a, b):
    ah, al = _split(a)
    bh, bl = _split(b)
    return _dot(ah, bh) + (_dot(ah, bl) + _dot(al, bh))


def _dot_sel(m_bf16, x):
    x1 = x.astype(BF16)
    r1 = x - x1.astype(F32)
    x2 = r1.astype(BF16)
    x3 = (r1 - x2.astype(F32)).astype(BF16)
    return _dot(m_bf16, x1) + (_dot(m_bf16, x2) + _dot(m_bf16, x3))


def _sigmoid(x):
    return 1.0 / (1.0 + jnp.exp(-x))


def _silu(x):
    return x * _sigmoid(x)


def _layer_norm_rows(x, g, b):
    mu = jnp.mean(x, axis=-1, keepdims=True)
    xc = x - mu
    var = jnp.mean(xc * xc, axis=-1, keepdims=True)
    return xc * lax.rsqrt(var + LN_EPS) * g + b


def _mm_kernel(x_ref, w_ref, o_ref):
    o_ref[...] = _dot(x_ref[...], w_ref[...]).astype(o_ref.dtype)


def matmul(x, w, out_dtype, tm=512, tn=512):
    M, K = x.shape
    N = w.shape[1]
    tm, tn = min(tm, M), min(tn, N)
    assert M % tm == 0 and N % tn == 0
    return pl.pallas_call(
        _mm_kernel,
        grid=(N // tn, M // tm),
        in_specs=[pl.BlockSpec((tm, K), lambda j, i: (i, 0)),
                  pl.BlockSpec((K, tn), lambda j, i: (0, j))],
        out_specs=pl.BlockSpec((tm, tn), lambda j, i: (i, j)),
        out_shape=jax.ShapeDtypeStruct((M, N), out_dtype),
        compiler_params=_cparams(("arbitrary", "arbitrary")),
        name="matmul",
    )(x, w)


def _mm_res_ln_kernel(y_ref, w_ref, h_ref, g_ref, b_ref, o32_ref, o16_ref):
    mix = _dot(y_ref[...], w_ref[...])
    out = _layer_norm_rows(DEEPNORM_ALPHA * h_ref[...] + mix, g_ref[...], b_ref[...])
    o32_ref[...] = out
    o16_ref[...] = out.astype(BF16)


def matmul_residual_ln(y, w, h, g, b, tm=512):
    M, K = y.shape
    D = w.shape[1]
    tm = min(tm, M)
    assert M % tm == 0
    return pl.pallas_call(
        _mm_res_ln_kernel,
        grid=(M // tm,),
        in_specs=[pl.BlockSpec((tm, K), lambda i: (i, 0)),
                  pl.BlockSpec((K, D), lambda i: (0, 0)),
                  pl.BlockSpec((tm, D), lambda i: (i, 0)),
                  pl.BlockSpec((1, D), lambda i: (0, 0)),
                  pl.BlockSpec((1, D), lambda i: (0, 0))],
        out_specs=[pl.BlockSpec((tm, D), lambda i: (i, 0)),
                   pl.BlockSpec((tm, D), lambda i: (i, 0))],
        out_shape=[jax.ShapeDtypeStruct((M, D), F32), jax.ShapeDtypeStruct((M, D), BF16)],
        compiler_params=_cparams(("arbitrary",)),
        name="matmul_residual_ln",
    )(y, w, h, g.reshape(1, D), b.reshape(1, D))


def _ffn_kernel(x_ref, wg_ref, wu_ref, wd_ref, h_ref, g_ref, b_ref, o32_ref, o16_ref, acc_ref):
    f = pl.program_id(1)

    @pl.when(f == 0)
    def _():
        acc_ref[...] = jnp.zeros_like(acc_ref)

    x = x_ref[...]
    gate = _dot(x, wg_ref[...])
    up = _dot(x, wu_ref[...])
    hid = (_silu(gate) * up).astype(BF16)
    acc_ref[...] += _dot(hid, wd_ref[...])

    @pl.when(f == pl.num_programs(1) - 1)
    def _():
        out = _layer_norm_rows(DEEPNORM_ALPHA * h_ref[...] + acc_ref[...], g_ref[...], b_ref[...])
        o32_ref[...] = out
        o16_ref[...] = out.astype(BF16)


def swiglu_residual_ln(x16, h, wg, wu, wd, g, b, tm=512, tf=512):
    M, D = x16.shape
    FF = wg.shape[1]
    tm = min(tm, M)
    tf = min(tf, FF)
    assert M % tm == 0 and FF % tf == 0
    return pl.pallas_call(
        _ffn_kernel,
        grid=(M // tm, FF // tf),
        in_specs=[pl.BlockSpec((tm, D), lambda i, f: (i, 0)),
                  pl.BlockSpec((D, tf), lambda i, f: (0, f)),
                  pl.BlockSpec((D, tf), lambda i, f: (0, f)),
                  pl.BlockSpec((tf, D), lambda i, f: (f, 0)),
                  pl.BlockSpec((tm, D), lambda i, f: (i, 0)),
                  pl.BlockSpec((1, D), lambda i, f: (0, 0)),
                  pl.BlockSpec((1, D), lambda i, f: (0, 0))],
        out_specs=[pl.BlockSpec((tm, D), lambda i, f: (i, 0)),
                   pl.BlockSpec((tm, D), lambda i, f: (i, 0))],
        out_shape=[jax.ShapeDtypeStruct((M, D), F32), jax.ShapeDtypeStruct((M, D), BF16)],
        scratch_shapes=[pltpu.VMEM((tm, D), F32)],
        compiler_params=_cparams(("arbitrary", "arbitrary")),
        name="swiglu_residual_ln",
    )(x16, wg, wu, wd, h, g.reshape(1, D), b.reshape(1, D))


def _moe_dense_kernel(x_ref, c_ref, wg_ref, wu_ref, wd_ref, h_ref, g_ref, b_ref,
                      o32_ref, o16_ref, acc_ref):
    e = pl.program_id(1)
    f = pl.program_id(2)

    @pl.when((e == 0) & (f == 0))
    def _():
        acc_ref[...] = jnp.zeros_like(acc_ref)

    x = x_ref[...]
    c = c_ref[...]
    lane = lax.broadcasted_iota(jnp.int32, c.shape, 1)
    c_e = jnp.sum(jnp.where(lane == e, c, 0.0), axis=-1, keepdims=True)
    gate = _dot(x, wg_ref[...])
    up = _dot(x, wu_ref[...])
    hid = (_silu(gate) * up).astype(BF16)
    acc_ref[...] += c_e * _dot(hid, wd_ref[...])

    @pl.when((e == pl.num_programs(1) - 1) & (f == pl.num_programs(2) - 1))
    def _():
        out = _layer_norm_rows(DEEPNORM_ALPHA * h_ref[...] + acc_ref[...], g_ref[...], b_ref[...])
        o32_ref[...] = out
        o16_ref[...] = out.astype(BF16)


def moe_dense_residual_ln(x16, h, comb, wg, wu, wd, g, b, tm=512, tf=512):
    M, D = x16.shape
    E, _, FF = wg.shape
    tm = min(tm, M)
    tf = min(tf, FF)
    assert M % tm == 0 and FF % tf == 0
    return pl.pallas_call(
        _moe_dense_kernel,
        grid=(M // tm, E, FF // tf),
        in_specs=[pl.BlockSpec((tm, D), lambda i, e, f: (i, 0)),
                  pl.BlockSpec((tm, LANES), lambda i, e, f: (i, 0)),
                  pl.BlockSpec((None, D, tf), lambda i, e, f: (e, 0, f)),
                  pl.BlockSpec((None, D, tf), lambda i, e, f: (e, 0, f)),
                  pl.BlockSpec((None, tf, D), lambda i, e, f: (e, f, 0)),
                  pl.BlockSpec((tm, D), lambda i, e, f: (i, 0)),
                  pl.BlockSpec((1, D), lambda i, e, f: (0, 0)),
                  pl.BlockSpec((1, D), lambda i, e, f: (0, 0))],
        out_specs=[pl.BlockSpec((tm, D), lambda i, e, f: (i, 0)),
                   pl.BlockSpec((tm, D), lambda i, e, f: (i, 0))],
        out_shape=[jax.ShapeDtypeStruct((M, D), F32), jax.ShapeDtypeStruct((M, D), BF16)],
        scratch_shapes=[pltpu.VMEM((tm, D), F32)],
        compiler_params=_cparams(("arbitrary", "arbitrary", "arbitrary")),
        name="moe_dense_residual_ln",
    )(x16, comb, wg, wu, wd, h, g.reshape(1, D), b.reshape(1, D))


def _router_kernel(h_ref, w_ref, comb_ref):
    logits = _dot3(h_ref[...], w_ref[...])
    lane = lax.broadcasted_iota(jnp.int32, logits.shape, 1)
    neg = jnp.float32(-jnp.inf)
    logits = jnp.where(lane < N_EXPERTS, logits, neg)
    m1 = jnp.max(logits, axis=-1, keepdims=True)
    i1 = jnp.min(jnp.where(logits == m1, lane, LANES), axis=-1, keepdims=True)
    rest = jnp.where(lane == i1, neg, logits)
    m2 = jnp.max(rest, axis=-1, keepdims=True)
    i2 = jnp.min(jnp.where(rest == m2, lane, LANES), axis=-1, keepdims=True)
    e2 = jnp.exp(m2 - m1)
    den = 1.0 + e2
    comb_ref[...] = jnp.where(lane == i1, 1.0 / den, 0.0) + jnp.where(lane == i2, e2 / den, 0.0)


def moe_router_combine(h, router, tm=512):
    M, D = h.shape
    E = router.shape[1]
    tm = min(tm, M)
    w = jnp.zeros((D, LANES), F32).at[:, :E].set(router)
    return pl.pallas_call(
        _router_kernel,
        grid=(M // tm,),
        in_specs=[pl.BlockSpec((tm, D), lambda i: (i, 0)),
                  pl.BlockSpec((D, LANES), lambda i: (0, 0))],
        out_specs=pl.BlockSpec((tm, LANES), lambda i: (i, 0)),
        out_shape=jax.ShapeDtypeStruct((M, LANES), F32),
        compiler_params=_cparams(("arbitrary",)),
        name="moe_router",
    )(h, w)


def _sb_kernel(q_ref, k_ref, v_ref, o_ref, *, scale):
    S = q_ref.shape[1]
    n_sub = SB_TQ // SB_TK
    row = lax.broadcasted_iota(jnp.int32, (SB_TK, SB_TK), 0)
    col = lax.broadcasted_iota(jnp.int32, (SB_TK, SB_TK), 1)
    suffix = jnp.where(row > col, 1.0, 0.0).astype(BF16)
    t_loc = lax.broadcasted_iota(jnp.int32, (SB_TQ, SB_TK), 0)
    s_loc = lax.broadcasted_iota(jnp.int32, (SB_TQ, SB_TK), 1)

    def key_block(q, kb, carry, causal):
        acc, run = carry
        ks = pl.multiple_of(kb * SB_TK, SB_TK)
        k = k_ref[0, pl.ds(ks, SB_TK), :]
        v = v_ref[0, pl.ds(ks, SB_TK), :]
        z = _dot_nt(q, k) * scale
        log_sig = jnp.minimum(z, 0.0) - jnp.log1p(jnp.exp(-jnp.abs(z)))
        log_keep = log_sig - z
        if causal is not None:
            log_keep = jnp.where(causal, log_keep, 0.0)
        hi, lo = _split(log_keep)
        later = _dot(hi, suffix) + _dot(lo, suffix) + run
        attn = jnp.exp(log_sig + later)
        if causal is not None:
            attn = jnp.where(causal, attn, 0.0)
        acc = acc + _dot(attn.astype(BF16), v)
        run = run + jnp.sum(log_keep, axis=-1, keepdims=True)
        return acc, run

    def q_tile(qi, _):
        qs = pl.multiple_of(qi * SB_TQ, SB_TQ)
        q = q_ref[0, pl.ds(qs, SB_TQ), :]
        carry = (jnp.zeros((SB_TQ, HEAD_DIM), F32), jnp.zeros((SB_TQ, 1), F32))
        for sub in reversed(range(n_sub)):
            causal = (s_loc + sub * SB_TK) < t_loc
            carry = key_block(q, qi * n_sub + sub, carry, causal)

        def cond(st):
            kb, _, run = st
            return (kb >= 0) & (jnp.max(run) > EXP_UNDERFLOW)

        def body(st):
            kb, acc, run = st
            acc, run = key_block(q, kb, (acc, run), None)
            return kb - 1, acc, run

        _, acc, _ = lax.while_loop(cond, body, (qi * n_sub - 1, carry[0], carry[1]))
        o_ref[0, pl.ds(qs, SB_TQ), :] = acc.astype(o_ref.dtype)
        return 0

    lax.fori_loop(0, S // SB_TQ, q_tile, 0)


def stick_breaking_attention(proj, n_heads):
    B, S, _ = proj.shape
    H = n_heads
    assert S % SB_TQ == 0
    blk = (1, S, HEAD_DIM)
    return pl.pallas_call(
        functools.partial(_sb_kernel, scale=HEAD_DIM ** -0.5),
        grid=(B, H),
        in_specs=[pl.BlockSpec(blk, lambda b, h: (b, 0, h)),
                  pl.BlockSpec(blk, lambda b, h: (b, 0, H + h)),
                  pl.BlockSpec(blk, lambda b, h: (b, 0, 2 * H + h))],
        out_specs=pl.BlockSpec(blk, lambda b, h: (b, 0, h)),
        out_shape=jax.ShapeDtypeStruct((B, S, H * HEAD_DIM), BF16),
        compiler_params=_cparams(("arbitrary", "arbitrary")),
        name="stick_breaking_attention",
    )(proj, proj, proj)


def _gdn_prep_kernel(x_ref, halo_ref, ba_ref, cw_ref, alog_ref, dtb_ref,
                     q_ref, k_ref, v_ref, gc_ref, beta_ref, *, n_qk, n_v):
    t = pl.program_id(1)
    x = x_ref[0]
    ts = x.shape[0]
    halo = jnp.where(t > 0, halo_ref[0], 0.0)
    xp = jnp.concatenate([halo, x], axis=0)
    cw = cw_ref[...]
    y = cw[GDN_CONV_K - 1:GDN_CONV_K, :] * x
    for j in range(GDN_CONV_K - 1):
        off = 8 - (GDN_CONV_K - 1) + j
        y = y + cw[j:j + 1, :] * xp[off:off + ts, :]
    y = _silu(y)

    ba = ba_ref[0]
    beta = _sigmoid(ba[:, :n_v])
    a = ba[:, n_v:2 * n_v] + dtb_ref[...]
    softplus = jnp.maximum(a, 0.0) + jnp.log1p(jnp.exp(-jnp.abs(a)))
    g = -jnp.exp(alog_ref[...]) * softplus
    r = lax.broadcasted_iota(jnp.int32, (ts, ts), 0)
    c = lax.broadcasted_iota(jnp.int32, (ts, ts), 1)
    same_chunk_prefix = (r // GDN_CHUNK == c // GDN_CHUNK) & (c <= r)
    gc_ref[0] = _dot_sel(jnp.where(same_chunk_prefix, 1.0, 0.0).astype(BF16), g)
    beta_ref[0] = beta

    key_dim = n_qk * HEAD_DIM
    for h in range(n_qk):
        sl = slice(h * HEAD_DIM, (h + 1) * HEAD_DIM)
        qh = y[:, sl]
        q_ref[0, :, sl] = (qh * lax.rsqrt(jnp.sum(qh * qh, -1, keepdims=True) + RMS_EPS)
                           * (HEAD_DIM ** -0.5)).astype(BF16)
        kh = y[:, key_dim + h * HEAD_DIM:key_dim + (h + 1) * HEAD_DIM]
        k_ref[0, :, sl] = (kh * lax.rsqrt(jnp.sum(kh * kh, -1, keepdims=True) + RMS_EPS)).astype(BF16)
    for h in range(n_v):
        sl = slice(h * HEAD_DIM, (h + 1) * HEAD_DIM)
        v_ref[0, :, sl] = (y[:, 2 * key_dim + h * HEAD_DIM:2 * key_dim + (h + 1) * HEAD_DIM]
                           * beta[:, h:h + 1]).astype(BF16)


def gdn_prepare(proj, ba, conv_w, a_log, dt_bias, n_qk, n_v, ts=256):
    B, S, _ = proj.shape
    conv_dim = (2 * n_qk + n_v) * HEAD_DIM
    ts = min(ts, S)
    assert S % ts == 0 and ts % GDN_CHUNK == 0
    halo_blocks = ts // 8
    return pl.pallas_call(
        functools.partial(_gdn_prep_kernel, n_qk=n_qk, n_v=n_v),
        grid=(B, S // ts),
        in_specs=[pl.BlockSpec((1, ts, conv_dim), lambda b, t: (b, t, 0)),
                  pl.BlockSpec((1, 8, conv_dim), lambda b, t: (b, jnp.maximum(t * halo_blocks - 1, 0), 0)),
                  pl.BlockSpec((1, ts, LANES), lambda b, t: (b, t, 0)),
                  pl.BlockSpec((GDN_CONV_K, conv_dim), lambda b, t: (0, 0)),
                  pl.BlockSpec((1, n_v), lambda b, t: (0, 0)),
                  pl.BlockSpec((1, n_v), lambda b, t: (0, 0))],
        out_specs=[pl.BlockSpec((1, ts, n_qk * HEAD_DIM), lambda b, t: (b, t, 0)),
                   pl.BlockSpec((1, ts, n_qk * HEAD_DIM), lambda b, t: (b, t, 0)),
                   pl.BlockSpec((1, ts, n_v * HEAD_DIM), lambda b, t: (b, t, 0)),
                   pl.BlockSpec((1, ts, n_v), lambda b, t: (b, t, 0)),
                   pl.BlockSpec((1, ts, n_v), lambda b, t: (b, t, 0))],
        out_shape=[jax.ShapeDtypeStruct((B, S, n_qk * HEAD_DIM), BF16),
                   jax.ShapeDtypeStruct((B, S, n_qk * HEAD_DIM), BF16),
                   jax.ShapeDtypeStruct((B, S, n_v * HEAD_DIM), BF16),
                   jax.ShapeDtypeStruct((B, S, n_v), F32),
                   jax.ShapeDtypeStruct((B, S, n_v), F32)],
        compiler_params=_cparams(("arbitrary", "arbitrary")),
        name="gdn_prepare",
    )(proj, proj, ba, conv_w, a_log.reshape(1, n_v), dt_bias.reshape(1, n_v))


def _unit_lower_inverse(low):
    C = low.shape[0]
    r = lax.broadcasted_iota(jnp.int32, (C, C), 0)
    c = lax.broadcasted_iota(jnp.int32, (C, C), 1)
    eye = jnp.where(r == c, 1.0, 0.0)

    def off_block(s):
        return (r // (2 * s) == c // (2 * s)) & ((r // s) % 2 == 1) & ((c // s) % 2 == 0)

    inv = eye - jnp.where(off_block(1), low, 0.0)
    s = 2
    while s < C:
        m = jnp.where(off_block(s), low, 0.0)
        inv = inv - _dot3(inv, _dot3(m, inv))
        s *= 2
    return inv


def _gdn_chunk_kernel(q_ref, k_ref, vb_ref, z_ref, gc_ref, gct_ref, beta_ref, nw_ref,
                      o_ref, state_ref, *, n_qk, n_v):
    @pl.when(pl.program_id(1) == 0)
    def _():
        state_ref[...] = jnp.zeros_like(state_ref)

    C = GDN_CHUNK
    rep = n_v // n_qk
    gc = gc_ref[0]
    gct = gct_ref[0, 0]
    beta = beta_ref[0]
    g_last = gc[C - 1:C, :]
    exp_gc = jnp.exp(gc)
    beta_exp_gc = beta * exp_gc
    exp_rest = jnp.exp(g_last - gc)
    exp_last = jnp.exp(g_last)
    r = lax.broadcasted_iota(jnp.int32, (C, C), 0)
    c = lax.broadcasted_iota(jnp.int32, (C, C), 1)
    tri = c <= r
    strict = c < r
    nw = nw_ref[...]

    for hq in range(n_qk):
        qsl = slice(hq * HEAD_DIM, (hq + 1) * HEAD_DIM)
        q = q_ref[0, :, qsl]
        k = k_ref[0, :, qsl]
        qf = q.astype(F32)
        kf = k.astype(F32)
        kk = _dot_nt(k, k)
        qk = _dot_nt(q, k)
        for hv in range(hq * rep, (hq + 1) * rep):
            vsl = slice(hv * HEAD_DIM, (hv + 1) * HEAD_DIM)
            col = slice(hv, hv + 1)
            diff = gc[:, col] - gct[hv:hv + 1, :]
            decay = jnp.where(tri, jnp.exp(jnp.where(tri, diff, 0.0)), 0.0)
            low = jnp.where(strict, beta[:, col] * kk * decay, 0.0)
            intra = qk * decay
            t_inv = _unit_lower_inverse(low).astype(BF16)
            u = _dot(t_inv, vb_ref[0, :, vsl])
            w = _dot(t_inv, (kf * beta_exp_gc[:, col]).astype(BF16))
            state = state_ref[hv]
            state16 = state.astype(BF16)
            v_new = u - _dot(w.astype(BF16), state16)
            v_new16 = v_new.astype(BF16)
            o = (_dot((qf * exp_gc[:, col]).astype(BF16), state16)
                 + _dot(intra.astype(BF16), v_new16))
            k_dec = (kf * exp_rest[:, col]).T.astype(BF16)
            state_ref[hv] = state * exp_last[:, col] + _dot(k_dec, v_new16)
            o = o * lax.rsqrt(jnp.mean(o * o, axis=-1, keepdims=True) + RMS_EPS) * nw
            o_ref[0, :, vsl] = (o * _silu(z_ref[0, :, vsl])).astype(o_ref.dtype)


def gdn_chunk_scan(q, k, vb, proj, z_block, gc, beta, norm_w, n_qk, n_v):
    B, S, _ = q.shape
    C = GDN_CHUNK
    N = S // C
    gct = jnp.swapaxes(gc.reshape(B, N, C, n_v), 2, 3)
    return pl.pallas_call(
        functools.partial(_gdn_chunk_kernel, n_qk=n_qk, n_v=n_v),
        grid=(B, N),
        in_specs=[pl.BlockSpec((1, C, n_qk * HEAD_DIM), lambda b, n: (b, n, 0)),
                  pl.BlockSpec((1, C, n_qk * HEAD_DIM), lambda b, n: (b, n, 0)),
                  pl.BlockSpec((1, C, n_v * HEAD_DIM), lambda b, n: (b, n, 0)),
                  pl.BlockSpec((1, C, n_v * HEAD_DIM), lambda b, n: (b, n, z_block)),
                  pl.BlockSpec((1, C, n_v), lambda b, n: (b, n, 0)),
                  pl.BlockSpec((1, 1, n_v, C), lambda b, n: (b, n, 0, 0)),
                  pl.BlockSpec((1, C, n_v), lambda b, n: (b, n, 0)),
                  pl.BlockSpec((1, HEAD_DIM), lambda b, n: (0, 0))],
        out_specs=pl.BlockSpec((1, C, n_v * HEAD_DIM), lambda b, n: (b, n, 0)),
        out_shape=jax.ShapeDtypeStruct((B, S, n_v * HEAD_DIM), BF16),
        scratch_shapes=[pltpu.VMEM((n_v, HEAD_DIM, HEAD_DIM), F32)],
        compiler_params=_cparams(("arbitrary", "arbitrary")),
        name="gdn_chunk_scan",
    )(q, k, vb, proj, gc, gct, beta, norm_w.reshape(1, HEAD_DIM))


def gated_deltanet_mix(h16, B, S, w_in, conv_w, a_log, dt_bias, norm_w):
    D = h16.shape[1]
    n_v = a_log.shape[0]
    n_qk = (conv_w.shape[1] // HEAD_DIM - n_v) // 2
    conv_dim = conv_w.shape[1]
    val_dim = n_v * HEAD_DIM
    w_main = w_in[:, :conv_dim + val_dim].astype(BF16)
    w_ba = jnp.zeros((D, LANES), BF16).at[:, :2 * n_v].set(w_in[:, conv_dim + val_dim:].astype(BF16))
    proj = matmul(h16, w_main, F32).reshape(B, S, conv_dim + val_dim)
    ba = matmul(h16, w_ba, F32).reshape(B, S, LANES)
    q, k, vb, gc, beta = gdn_prepare(proj, ba, conv_w, a_log, dt_bias, n_qk, n_v)
    assert conv_dim % val_dim == 0
    o = gdn_chunk_scan(q, k, vb, proj, conv_dim // val_dim, gc, beta, norm_w, n_qk, n_v)
    return o.reshape(B * S, val_dim)


def kernel(x, gdn_w_in, gdn_conv_w, gdn_a_log, gdn_dt_bias, gdn_norm_w, gdn_w_out, sb_w_in, sb_w_out, ffn_w_gate, ffn_w_up, ffn_w_down, moe_router, moe_w_gate, moe_w_up, moe_w_down, ln_g, ln_b):
    B, S, D = x.shape
    h = x.reshape(B * S, D)
    h16 = h.astype(BF16)
    for i in range(DEPTH):
        j = i // 2
        if i % 2 == 0:
            mix = gated_deltanet_mix(h16, B, S, gdn_w_in[j], gdn_conv_w[j], gdn_a_log[j],
                                     gdn_dt_bias[j], gdn_norm_w[j])
            w_out = gdn_w_out[j]
        else:
            n_heads = sb_w_out.shape[1] // HEAD_DIM
            proj = matmul(h16, sb_w_in[j].astype(BF16), BF16).reshape(B, S, -1)
            mix = stick_breaking_attention(proj, n_heads).reshape(B * S, -1)
            w_out = sb_w_out[j]
        h, h16 = matmul_residual_ln(mix, w_out.astype(BF16), h, ln_g[i, 0], ln_b[i, 0])
        if i % 2 == 0:
            h, h16 = swiglu_residual_ln(h16, h, ffn_w_gate[j].astype(BF16), ffn_w_up[j].astype(BF16),
                                        ffn_w_down[j].astype(BF16), ln_g[i, 1], ln_b[i, 1])
        else:
            comb = moe_router_combine(h, moe_router[j])
            h, h16 = moe_dense_residual_ln(h16, h, comb, moe_w_gate[j].astype(BF16),
                                           moe_w_up[j].astype(BF16), moe_w_down[j].astype(BF16),
                                           ln_g[i, 1], ln_b[i, 1])
    return h.reshape(B, S, D)
```
